```python
import jax, jax.numpy as jnp
from jax import lax
import numpy as np

D_MODEL = 2048
BATCH = 2
SEQ = 16384
DEPTH = 1

N_META = 16
D_MIX = 2 * D_MODEL
D_CONV = D_MIX // 2
CONV_GROUPS = 16
SHORT_CONV_W = 3
D_SSM = D_MIX - D_CONV
SSM_HEAD_DIM = 64
SSM_HEADS = D_SSM // SSM_HEAD_DIM
SSM_GROUPS = 8
SSM_HEADS_PER_GROUP = SSM_HEADS // SSM_GROUPS
SSM_STATE = 128
SSM_CONV_W = 4
CHUNK = 128
SSD_FRONT_PAD = CHUNK - N_META
D_XBC = D_SSM + 2 * SSM_GROUPS * SSM_STATE
D_IN_PROJ = 3 * D_CONV + D_SSM + D_XBC + SSM_HEADS
D_FF = 4 * D_MODEL
EPS = 1e-6
DT_MIN = 1e-3
DT_MAX = 1e-1

kernel_name = "hymba_shortconv_ssd_hybrid_layer"


def rms_norm(x, gain):
    xf = x.astype(jnp.float32)
    y = xf * lax.rsqrt(jnp.mean(xf * xf, axis=-1, keepdims=True) + EPS)
    return (y * gain.astype(jnp.float32)).astype(x.dtype)


def grouped_rms_norm(x, gain, n_groups):
    lead = x.shape[:-1]
    d = x.shape[-1]
    xg = x.astype(jnp.float32).reshape(*lead, n_groups, d // n_groups)
    xg = xg * lax.rsqrt(jnp.mean(xg * xg, axis=-1, keepdims=True) + EPS)
    return (xg.reshape(*lead, d) * gain.astype(jnp.float32)).astype(x.dtype)


def causal_depthwise_conv(x, w):
    k_w = w.shape[0]
    length = x.shape[1]
    xp = jnp.pad(x, ((0, 0), (k_w - 1, 0), (0, 0)))
    y = w[k_w - 1] * x
    for k in range(k_w - 1):
        y = y + w[k] * xp[:, k:k + length]
    return y


def pad_front(a, n):
    return jnp.pad(a, [(0, 0), (n, 0)] + [(0, 0)] * (a.ndim - 2))


def segsum_exp(a):
    q = a.shape[-1]
    cs = jnp.cumsum(a, axis=-1)
    diff = cs[..., :, None] - cs[..., None, :]
    mask = jnp.tril(jnp.ones((q, q), dtype=bool))
    return jnp.exp(jnp.where(mask, diff, -jnp.inf))


def ssd_chunked(x_dt, a_dt, b, c):
    bsz, t_len = x_dt.shape[:2]
    nc = t_len // CHUNK
    g, r, p, n = SSM_GROUPS, SSM_HEADS_PER_GROUP, SSM_HEAD_DIM, SSM_STATE
    x = x_dt.reshape(bsz, nc, CHUNK, g, r, p)
    b = b.reshape(bsz, nc, CHUNK, g, n)
    c = c.reshape(bsz, nc, CHUNK, g, n)
    a = a_dt.reshape(bsz, nc, CHUNK, g, r).transpose(0, 3, 4, 1, 2)
    a_cs = jnp.cumsum(a, axis=-1)

    decay_mat = segsum_exp(a)
    cb = jnp.einsum("bclgn,bcsgn->bgcls", c, b)
    scores = cb[:, :, None] * decay_mat
    y_diag = jnp.einsum("bgrcls,bcsgrp->bclgrp", scores, x)

    decay_to_end = jnp.exp(a_cs[..., -1:] - a_cs).transpose(0, 3, 4, 1, 2)
    states = jnp.einsum("bclgn,bclgrp->bcgrpn", b, x * decay_to_end[..., None])

    chunk_decay = jnp.exp(a_cs[..., -1]).transpose(3, 0, 1, 2)

    def step(h, inp):
        dec, s = inp
        return h * dec[..., None, None] + s, h

    h0 = jnp.zeros((bsz, g, r, p, n), states.dtype)
    _, prev_states = lax.scan(step, h0, (chunk_decay, states.transpose(1, 0, 2, 3, 4, 5)))

    decay_from_start = jnp.exp(a_cs).transpose(0, 3, 4, 1, 2)
    y_off = jnp.einsum("bclgn,cbgrpn->bclgrp", c, prev_states) * decay_from_start[..., None]
    return (y_diag + y_off).reshape(bsz, t_len, SSM_HEADS, p)


def hybrid_mixer(h, w_in, short_conv_w, conv_norm_g, ssm_conv_w, ssm_conv_b,
                 dt_bias, a_log, d_skip, ssm_norm_g, w_out):
    bsz, length, _ = h.shape
    proj = h @ w_in
    o1 = D_CONV
    o2 = 2 * D_CONV
    o3 = 3 * D_CONV
    o4 = o3 + D_SSM
    o5 = o4 + D_XBC
    gate_b, gate_c, v, z, xbc, dt_raw = jnp.split(proj, [o1, o2, o3, o4, o5], axis=-1)

    y_conv = gate_b * causal_depthwise_conv(gate_c * v, short_conv_w)
    y_conv = grouped_rms_norm(y_conv, conv_norm_g, CONV_GROUPS)

    xbc = jax.nn.silu(causal_depthwise_conv(xbc, ssm_conv_w) + ssm_conv_b)
    xs, bs, cs = jnp.split(xbc, [D_SSM, D_SSM + SSM_GROUPS * SSM_STATE], axis=-1)
    xs = xs.reshape(bsz, length, SSM_HEADS, SSM_HEAD_DIM)
    bs = bs.reshape(bsz, length, SSM_GROUPS, SSM_STATE)
    cs = cs.reshape(bsz, length, SSM_GROUPS, SSM_STATE)
    dt = jax.nn.softplus(dt_raw.astype(jnp.float32) + dt_bias.astype(jnp.float32))
    a_neg = -jnp.exp(a_log.astype(jnp.float32))
    x_dt = xs * dt[..., None].astype(xs.dtype)
    a_dt = dt * a_neg
    y_ssm = ssd_chunked(pad_front(x_dt, SSD_FRONT_PAD), pad_front(a_dt, SSD_FRONT_PAD),
                        pad_front(bs, SSD_FRONT_PAD), pad_front(cs, SSD_FRONT_PAD))
    y_ssm = y_ssm[:, SSD_FRONT_PAD:].astype(xs.dtype) + d_skip[:, None] * xs
    y_ssm = y_ssm.reshape(bsz, length, D_SSM)
    y_ssm = grouped_rms_norm(y_ssm * jax.nn.silu(z), ssm_norm_g, SSM_GROUPS)

    return jnp.concatenate([y_conv, y_ssm], axis=-1) @ w_out


def squared_relu_mlp(h, w_ff1, w_ff2):
    return jnp.square(jax.nn.relu(h @ w_ff1)) @ w_ff2


def setup_inputs(seed: int = 0) -> dict:
    key = jax.random.key(seed)
    ks = jax.random.split(key, 18)
    f32 = jnp.float32

    def gain(k, d):
        return 1.0 + 0.02 * jax.random.normal(k, (DEPTH, d), f32)

    dt_init = jnp.exp(jax.random.uniform(ks[7], (DEPTH, SSM_HEADS), f32,
                                         minval=np.log(DT_MIN), maxval=np.log(DT_MAX)))
    dt_bias = dt_init + jnp.log(-jnp.expm1(-dt_init))
    return {
        "x": jax.random.normal(ks[0], (BATCH, SEQ, D_MODEL), f32),
        "meta_tokens": jax.random.normal(ks[1], (N_META, D_MODEL), f32),
        "w_in": jax.random.normal(ks[2], (DEPTH, D_MODEL, D_IN_PROJ), f32) * D_MODEL ** -0.5,
        "short_conv_w": jax.random.normal(ks[3], (DEPTH, SHORT_CONV_W, D_CONV), f32) * SHORT_CONV_W ** -0.5,
        "conv_norm_g": gain(ks[4], D_CONV),
        "ssm_conv_w": jax.random.normal(ks[5], (DEPTH, SSM_CONV_W, D_XBC), f32) * SSM_CONV_W ** -0.5,
        "ssm_conv_b": 0.02 * jax.random.normal(ks[6], (DEPTH, D_XBC), f32),
        "dt_bias": dt_bias,
        "a_log": jnp.log(jax.random.uniform(ks[8], (DEPTH, SSM_HEADS), f32, minval=1.0, maxval=16.0)),
        "d_skip": 1.0 + 0.02 * jax.random.normal(ks[9], (DEPTH, SSM_HEADS), f32),
        "ssm_norm_g": gain(ks[10], D_SSM),
        "w_out": jax.random.normal(ks[11], (DEPTH, D_MIX, D_MODEL), f32) * D_MIX ** -0.5,
        "pre_mix_g": gain(ks[12], D_MODEL),
        "post_mix_g": gain(ks[13], D_MODEL),
        "pre_mlp_g": gain(ks[14], D_MODEL),
        "post_mlp_g": gain(ks[15], D_MODEL),
        "w_ff1": jax.random.normal(ks[16], (DEPTH, D_MODEL, D_FF), f32) * D_MODEL ** -0.5,
        "w_ff2": jax.random.normal(ks[17], (DEPTH, D_FF, D_MODEL), f32) * D_FF ** -0.5,
    }


def reference(x, meta_tokens, w_in, short_conv_w, conv_norm_g, ssm_conv_w, ssm_conv_b,
              dt_bias, a_log, d_skip, ssm_norm_g, w_out, pre_mix_g, post_mix_g,
              pre_mlp_g, post_mlp_g, w_ff1, w_ff2):
    in_dtype = x.dtype
    bsz = x.shape[0]
    meta = jnp.broadcast_to(meta_tokens[None].astype(in_dtype), (bsz, N_META, D_MODEL))
    h = jnp.concatenate([meta, x], axis=1)
    for i in range(DEPTH):
        mix = hybrid_mixer(rms_norm(h, pre_mix_g[i]), w_in[i], short_conv_w[i], conv_norm_g[i],
                           ssm_conv_w[i], ssm_conv_b[i], dt_bias[i], a_log[i], d_skip[i],
                           ssm_norm_g[i], w_out[i])
        h = h + rms_norm(mix, post_mix_g[i])
        ff = squared_relu_mlp(rms_norm(h, pre_mlp_g[i]), w_ff1[i], w_ff2[i])
        h = h + rms_norm(ff, post_mlp_g[i])
    return h[:, N_META:].astype(in_dtype)
```

```python
import functools

import jax
import jax.numpy as jnp
from jax import lax
from jax.experimental import pallas as pl
from jax.experimental.pallas import tpu as pltpu

F32 = jnp.float32
BF16 = jnp.bfloat16

D_MODEL = 2048
N_META = 16
D_CONV = 2048
CONV_GROUPS = 16
CONV_GROUP_W = D_CONV // CONV_GROUPS
D_SSM = 2048
HEAD_DIM = 64
N_HEADS = D_SSM // HEAD_DIM
SSM_GROUPS = 8
D_STATE = 128
CHUNK = 128
D_XBC = D_SSM + 2 * SSM_GROUPS * D_STATE
D_MAIN = 3 * D_CONV + D_SSM + D_XBC
D_FF = 4 * D_MODEL
EPS = 1e-6
LANES = 128
OFF_GB, OFF_GC, OFF_V, OFF_Z, OFF_XBC = 0, D_CONV, 2 * D_CONV, 3 * D_CONV, 3 * D_CONV + D_SSM
OFF_B = D_SSM
OFF_C = D_SSM + SSM_GROUPS * D_STATE
META_PAD = CHUNK - N_META
VMEM_LIMIT = 56 * 1024 * 1024


def _rms_scale(x):
    return lax.rsqrt(jnp.mean(x * x, axis=-1, keepdims=True) + EPS)


def _inproj_kernel(x_ref, g_ref, w_ref, wdt_ref, o_ref, dt_ref, xn_ref):
    @pl.when(pl.program_id(1) == 0)
    def _():
        x = x_ref[...]
        xn = (x * _rms_scale(x) * g_ref[...]).astype(BF16)
        xn_ref[...] = xn
        dt_ref[...] = jnp.dot(xn, wdt_ref[...], preferred_element_type=F32)

    o_ref[...] = jnp.dot(xn_ref[...], w_ref[...], preferred_element_type=F32).astype(o_ref.dtype)


def _inproj(x2d, g, w_main, w_dt, tm, tn):
    m = x2d.shape[0]
    return pl.pallas_call(
        _inproj_kernel,
        grid=(m // tm, D_MAIN // tn),
        in_specs=[
            pl.BlockSpec((tm, D_MODEL), lambda i, j: (i, 0)),
            pl.BlockSpec((1, D_MODEL), lambda i, j: (0, 0)),
            pl.BlockSpec((D_MODEL, tn), lambda i, j: (0, j)),
            pl.BlockSpec((D_MODEL, LANES), lambda i, j: (0, 0)),
        ],
        out_specs=[
            pl.BlockSpec((tm, tn), lambda i, j: (i, j)),
            pl.BlockSpec((tm, LANES), lambda i, j: (i, 0)),
        ],
        out_shape=[
            jax.ShapeDtypeStruct((m, D_MAIN), BF16),
            jax.ShapeDtypeStruct((m, LANES), F32),
        ],
        scratch_shapes=[pltpu.VMEM((tm, D_MODEL), BF16)],
        compiler_params=pltpu.CompilerParams(
            dimension_semantics=("arbitrary", "arbitrary"), vmem_limit_bytes=VMEM_LIMIT),
        name="inproj",
    )(x2d, g, w_main, w_dt)


def _split3(x):
    hi = x.astype(BF16)
    r1 = x - hi.astype(F32)
    lo = r1.astype(BF16)
    lo2 = (r1 - lo.astype(F32)).astype(BF16)
    return hi, lo, lo2


def _silu_mul(x):
    return x / (1.0 + jnp.exp(-x))


def _mixer_kernel(pm_ref, pr_ref, dm_ref, dr_ref, w3_ref, ga_ref, w4_ref, b4_ref, dtb_ref, aneg_ref,
                  dskip_ref, gs_ref, tri_ref, e3_ref, y_ref,
                  bufx, bufu, act, csb, state, cst_s, tot_s, dtt_s):
    c = pl.program_id(1)
    is_meta = c == 0

    @pl.when(is_meta)
    def _():
        bufx[0:8, :] = jnp.zeros((8, D_XBC), F32)
        bufu[0:8, :] = jnp.zeros((8, D_CONV), F32)
        state[...] = jnp.zeros_like(state)

    def ld(off, w):
        m = pm_ref[:, pl.ds(off, w)]
        r = pr_ref[0, :, pl.ds(off, w)]
        return jnp.where(is_meta, m, r).astype(F32)

    slab = 2 * LANES

    def part_a(s, carry):
        off = pl.multiple_of(s * slab, slab)
        cols = pl.ds(off, slab)
        u = ld(OFF_GC + off, slab) * ld(OFF_V + off, slab)
        bufu[8:8 + CHUNK, cols] = u
        conv = w3_ref[2:3, cols] * u
        conv = conv + w3_ref[0:1, cols] * bufu[6:6 + CHUNK, cols]
        conv = conv + w3_ref[1:2, cols] * bufu[7:7 + CHUNK, cols]
        bufu[0:8, cols] = bufu[CHUNK:CHUNK + 8, cols]
        ya = ld(OFF_GB + off, slab) * conv
        for gi in range(slab // CONV_GROUP_W):
            yg = ya[:, gi * CONV_GROUP_W:(gi + 1) * CONV_GROUP_W]
            gcols = pl.ds(off + gi * CONV_GROUP_W, CONV_GROUP_W)
            y_ref[0, :, gcols] = (yg * _rms_scale(yg) * ga_ref[:, gcols]).astype(y_ref.dtype)
        return carry

    lax.fori_loop(0, D_CONV // slab, part_a, 0)

    def part_b(s, carry):
        off = pl.multiple_of(s * slab, slab)
        cols = pl.ds(off, slab)
        x = ld(OFF_XBC + off, slab)
        bufx[8:8 + CHUNK, cols] = x
        cv = w4_ref[3:4, cols] * x
        cv = cv + w4_ref[0:1, cols] * bufx[5:5 + CHUNK, cols]
        cv = cv + w4_ref[1:2, cols] * bufx[6:6 + CHUNK, cols]
        cv = cv + w4_ref[2:3, cols] * bufx[7:7 + CHUNK, cols]
        bufx[0:8, cols] = bufx[CHUNK:CHUNK + 8, cols]
        act[:, cols] = _silu_mul(cv + b4_ref[:, cols])
        return carry

    lax.fori_loop(0, D_XBC // slab, part_b, 0)

    xb = jnp.where(is_meta, dm_ref[...], dr_ref[0]) + dtb_ref[...]
    dt = jnp.maximum(xb, 0.0) + jnp.log1p(jnp.exp(-jnp.abs(xb)))
    row = lax.broadcasted_iota(jnp.int32, (CHUNK, LANES), 0)
    col = lax.broadcasted_iota(jnp.int32, (CHUNK, LANES), 1)
    dt = jnp.where(jnp.logical_or(row >= META_PAD, jnp.logical_not(is_meta)), dt, 0.0)
    dtt = dt.T
    a_t = dtt * aneg_ref[...]
    tri = tri_ref[...]
    hi, lo, lo2 = _split3(a_t)
    cst = (jnp.dot(hi, tri, preferred_element_type=F32) + jnp.dot(lo, tri, preferred_element_type=F32)
           + jnp.dot(lo2, tri, preferred_element_type=F32))
    cs_t = cst[:, :CHUNK]
    cst_s[...] = cs_t
    tot_s[...] = cst[:, CHUNK:]
    dtt_s[...] = dtt
    chi, clo, clo2 = _split3(cs_t.T)
    csb[...] = jnp.dot(jnp.concatenate([chi, clo, clo2], axis=1), e3_ref[...], preferred_element_type=F32)

    causal = row >= col
    lane_lo = col < HEAD_DIM

    def group(g, carry):
        b_g = act[:, pl.ds(pl.multiple_of(OFF_B + g * D_STATE, LANES), D_STATE)]
        c_g = act[:, pl.ds(pl.multiple_of(OFF_C + g * D_STATE, LANES), D_STATE)]
        cb = lax.dot_general(c_g.astype(BF16), b_g.astype(BF16), (((1,), (1,)), ((), ())),
                             preferred_element_type=F32)
        b_t = b_g.T
        ys = []
        for pr in range(2):
            pair = 2 * g + pr
            xcols = pl.ds(pl.multiple_of(pair * LANES, LANES), LANES)
            xp = act[:, xcols]
            xpb = xp.astype(BF16)
            st = state[pair]
            stb = st.astype(BF16)
            zero = jnp.zeros_like(xpb)
            y = jnp.zeros((CHUNK, LANES), F32)
            new = jnp.zeros((D_STATE, LANES), F32)
            decs = []
            for r in range(2):
                h = 2 * pair + r
                keep = lane_lo if r == 0 else jnp.logical_not(lane_lo)
                xm = jnp.where(keep, xpb, zero)
                sm = jnp.where(keep, stb, zero)
                cs_col = csb[:, pl.ds(pl.multiple_of(h * LANES, LANES), LANES)]
                cs_row = cst_s[pl.ds(h, 1), :]
                dt_row = dtt_s[pl.ds(h, 1), :]
                tot_row = tot_s[pl.ds(h, 1), :]
                m_h = jnp.where(causal, jnp.exp(cs_col - cs_row), 0.0) * cb * dt_row
                c_dec = c_g * jnp.exp(cs_col)
                y = y + jnp.dot(m_h.astype(BF16), xm, preferred_element_type=F32)
                y = y + jnp.dot(c_dec.astype(BF16), sm, preferred_element_type=F32)
                w_row = dt_row * jnp.exp(tot_row - cs_row)
                new = new + jnp.dot((b_t * w_row).astype(BF16), xm, preferred_element_type=F32)
                decs.append(jnp.exp(tot_row))
            state[pair] = st * jnp.where(lane_lo[0:1], decs[0], decs[1]) + new
            z = ld(pl.multiple_of(OFF_Z + pair * LANES, LANES), LANES)
            ys.append((y + dskip_ref[:, xcols] * xp) * _silu_mul(z))
        yy = jnp.concatenate(ys, axis=1)
        ocols = pl.ds(pl.multiple_of(g * 2 * LANES, 2 * LANES), 2 * LANES)
        y_ref[0, :, pl.ds(pl.multiple_of(D_CONV + g * 2 * LANES, 2 * LANES), 2 * LANES)] = (
            yy * _rms_scale(yy) * gs_ref[:, ocols]).astype(y_ref.dtype)
        return carry

    lax.fori_loop(0, SSM_GROUPS, group, 0)


def _mixer(proj_meta, proj, dt_meta, dt_raw, w3, ga, w4, b4, dtb, aneg, dskip, gs, tri, e3):
    bsz, seq, _ = proj.shape
    n_chunks = seq // CHUNK

    def const(shape):
        return pl.BlockSpec(shape, lambda b, c: (0,) * len(shape))

    def chunked(width):
        return pl.BlockSpec((1, CHUNK, width), lambda b, c: (b, jnp.maximum(c - 1, 0), 0))

    return pl.pallas_call(
        _mixer_kernel,
        grid=(bsz, n_chunks + 1),
        in_specs=[
            const((CHUNK, D_MAIN)), chunked(D_MAIN), const((CHUNK, LANES)), chunked(LANES),
            const((8, D_CONV)), const((1, D_CONV)), const((8, D_XBC)), const((1, D_XBC)),
            const((1, LANES)), const((CHUNK, LANES)), const((1, D_SSM)), const((1, D_SSM)),
            const((CHUNK, 2 * CHUNK)), const((3 * LANES, N_HEADS * LANES)),
        ],
        out_specs=chunked(D_CONV + D_SSM),
        out_shape=jax.ShapeDtypeStruct((bsz, seq, D_CONV + D_SSM), BF16),
        scratch_shapes=[
            pltpu.VMEM((CHUNK + 8, D_XBC), F32),
            pltpu.VMEM((CHUNK + 8, D_CONV), F32),
            pltpu.VMEM((CHUNK, D_XBC), F32),
            pltpu.VMEM((CHUNK, N_HEADS * LANES), F32),
            pltpu.VMEM((N_HEADS // 2, D_STATE, LANES), F32),
            pltpu.VMEM((CHUNK, LANES), F32),
            pltpu.VMEM((CHUNK, LANES), F32),
            pltpu.VMEM((CHUNK, LANES), F32),
        ],
        compiler_params=pltpu.CompilerParams(
            dimension_semantics=("arbitrary", "arbitrary"), vmem_limit_bytes=VMEM_LIMIT),
        name="mixer",
    )(proj_meta, proj, dt_meta, dt_raw, w3, ga, w4, b4, dtb, aneg, dskip, gs, tri, e3)


def _outproj_kernel(y_ref, w_ref, x_ref, g_ref, o_ref):
    mix = jnp.dot(y_ref[...], w_ref[...], preferred_element_type=F32)
    o_ref[...] = x_ref[...] + mix * _rms_scale(mix) * g_ref[...]


def _outproj(y2d, w_out, x2d, g, tm):
    m = y2d.shape[0]
    d_mix = y2d.shape[1]
    return pl.pallas_call(
        _outproj_kernel,
        grid=(m // tm,),
        in_specs=[
            pl.BlockSpec((tm, d_mix), lambda i: (i, 0)),
            pl.BlockSpec((d_mix, D_MODEL), lambda i: (0, 0), pipeline_mode=pl.Buffered(1)),
            pl.BlockSpec((tm, D_MODEL), lambda i: (i, 0)),
            pl.BlockSpec((1, D_MODEL), lambda i: (0, 0)),
        ],
        out_specs=pl.BlockSpec((tm, D_MODEL), lambda i: (i, 0)),
        out_shape=jax.ShapeDtypeStruct((m, D_MODEL), F32),
        compiler_params=pltpu.CompilerParams(
            dimension_semantics=("arbitrary",), vmem_limit_bytes=VMEM_LIMIT),
        name="outproj",
    )(y2d, w_out, x2d, g)


def _mlp_kernel(h_ref, g1_ref, w1_ref, w2_ref, g2_ref, o_ref, hn_ref, acc_ref):
    f = pl.program_id(1)

    @pl.when(f == 0)
    def _():
        h = h_ref[...]
        hn_ref[...] = (h * _rms_scale(h) * g1_ref[...]).astype(BF16)
        acc_ref[...] = jnp.zeros_like(acc_ref)

    a = jnp.maximum(jnp.dot(hn_ref[...], w1_ref[...], preferred_element_type=F32), 0.0)
    acc_ref[...] += jnp.dot((a * a).astype(BF16), w2_ref[...], preferred_element_type=F32)

    @pl.when(f == pl.num_programs(1) - 1)
    def _():
        ff = acc_ref[...]
        o_ref[...] = h_ref[...] + ff * _rms_scale(ff) * g2_ref[...]


def _mlp(h2d, g1, w1, w2, g2, tm, tf):
    m = h2d.shape[0]
    return pl.pallas_call(
        _mlp_kernel,
        grid=(m // tm, D_FF // tf),
        in_specs=[
            pl.BlockSpec((tm, D_MODEL), lambda i, f: (i, 0)),
            pl.BlockSpec((1, D_MODEL), lambda i, f: (0, 0)),
            pl.BlockSpec((D_MODEL, tf), lambda i, f: (0, f)),
            pl.BlockSpec((tf, D_MODEL), lambda i, f: (f, 0)),
            pl.BlockSpec((1, D_MODEL), lambda i, f: (0, 0)),
        ],
        out_specs=pl.BlockSpec((tm, D_MODEL), lambda i, f: (i, 0)),
        out_shape=jax.ShapeDtypeStruct((m, D_MODEL), F32),
        scratch_shapes=[pltpu.VMEM((tm, D_MODEL), BF16), pltpu.VMEM((tm, D_MODEL), F32)],
        compiler_params=pltpu.CompilerParams(
            dimension_semantics=("arbitrary", "arbitrary"), vmem_limit_bytes=VMEM_LIMIT),
        name="mlp",
    )(h2d, g1, w1, w2, g2)


def _pad_rows(a, rows):
    return jnp.pad(a, ((0, rows - a.shape[0]), (0, 0)))


def _tile(m, want):
    t = min(m, want)
    assert m % t == 0, (m, t)
    return t


def _layer(x, meta, w_in, short_conv_w, conv_norm_g, ssm_conv_w, ssm_conv_b, dt_bias, a_log, d_skip,
           ssm_norm_g, w_out, pre_mix_g, post_mix_g, pre_mlp_g, post_mlp_g, w_ff1, w_ff2):
    bsz, seq, _ = x.shape
    assert seq % CHUNK == 0 and meta.shape == (N_META, D_MODEL)
    m = bsz * seq
    x2d = x.reshape(m, D_MODEL)

    w_main = w_in[:, :D_MAIN].astype(BF16)
    w_dt = jnp.pad(w_in[:, D_MAIN:], ((0, 0), (0, LANES - N_HEADS))).astype(BF16)
    g_mix = pre_mix_g.reshape(1, D_MODEL)

    proj, dt_raw = _inproj(x2d, g_mix, w_main, w_dt, _tile(m, 1024), 1024)
    proj_m, dt_m = _inproj(meta, g_mix, w_main, w_dt, N_META, 1024)
    proj_meta = jnp.pad(proj_m, ((META_PAD, 0), (0, 0)))
    dt_meta = jnp.pad(dt_m, ((META_PAD, 0), (0, 0)))

    a_neg = -jnp.exp(a_log.astype(F32))
    aneg = jnp.broadcast_to(_pad_rows(a_neg[:, None], CHUNK), (CHUNK, LANES))
    dtb = jnp.pad(dt_bias.astype(F32), (0, LANES - N_HEADS)).reshape(1, LANES)
    dskip = jnp.repeat(d_skip.astype(F32), HEAD_DIM).reshape(1, D_SSM)
    ti = jnp.arange(CHUNK)
    tri = jnp.concatenate([(ti[:, None] <= ti[None, :]), jnp.ones((CHUNK, CHUNK), bool)], axis=1).astype(BF16)
    expand = (jnp.arange(LANES)[:, None] == (jnp.arange(N_HEADS * LANES)[None, :] // LANES)).astype(BF16)
    e3 = jnp.concatenate([expand, expand, expand], axis=0)

    y = _mixer(proj_meta, proj.reshape(bsz, seq, D_MAIN), dt_meta, dt_raw.reshape(bsz, seq, LANES),
               _pad_rows(short_conv_w, 8), conv_norm_g.reshape(1, D_CONV), _pad_rows(ssm_conv_w, 8),
               ssm_conv_b.reshape(1, D_XBC), dtb, aneg, dskip, ssm_norm_g.reshape(1, D_SSM), tri, e3)

    h = _outproj(y.reshape(m, D_CONV + D_SSM), w_out.astype(BF16), x2d, post_mix_g.reshape(1, D_MODEL),
                 _tile(m, 512))
    out = _mlp(h, pre_mlp_g.reshape(1, D_MODEL), w_ff1.astype(BF16), w_ff2.astype(BF16),
               post_mlp_g.reshape(1, D_MODEL), _tile(m, 512), 1024)
    return out.reshape(bsz, seq, D_MODEL)


def kernel(x, meta_tokens, w_in, short_conv_w, conv_norm_g, ssm_conv_w, ssm_conv_b, dt_bias, a_log, d_skip,
           ssm_norm_g, w_out, pre_mix_g, post_mix_g, pre_mlp_g, post_mlp_g, w_ff1, w_ff2):
    assert w_in.shape[0] == 1, "single-layer trunk"
    return _layer(x, meta_tokens.astype(x.dtype), w_in[0], short_conv_w[0], conv_norm_g[0], ssm_conv_w[0],
                  ssm_conv_b[0], dt_bias[0], a_log[0], d_skip[0], ssm_norm_g[0], w_out[0], pre_mix_g[0],
                  post_mix_g[0], pre_mlp_g[0], post_mlp_g[0], w_ff1[0], w_ff2[0])
```
